```python
import math
import jax, jax.numpy as jnp
from jax import lax
import numpy as np

D_MODEL = 1024
BATCH = 16
SEQ = 2048
DEPTH = 4
DEC_BATCH = 16
DEC_SEQ = 4096
PAST_LEN = 128

N_MIXERS = 4
N_GROUPS = 4
GROUP_DIM = D_MODEL // N_GROUPS
POOL_WINDOWS = (2, 4, 8, 16)
CONV_WIDTH = 3
CHUNK = 128
GMLP_HALF = D_MODEL
GMLP_GROUP_DIM = GMLP_HALF // N_GROUPS
N_EXPERTS = 32
TOP_K = 4
D_EXPERT = D_MODEL
SWIGLU_ALPHA = 1.702
SWIGLU_LIMIT = 7.0
MOE_BLOCK = 512
LN_EPS = 1e-5
DEEPNORM_ALPHA = (2 * DEPTH) ** 0.25
DEEPNORM_BETA = (8 * DEPTH) ** -0.25

kernel_name = "hybrid_pool_conv_gmlp_fnet_moe_encoder"


def _n_layers_of(m):
    return len(range(m, DEPTH, N_MIXERS))


def layer_norm(x, g, b):
    xf = x.astype(jnp.float32)
    mu = jnp.mean(xf, axis=-1, keepdims=True)
    var = jnp.mean(jnp.square(xf - mu), axis=-1, keepdims=True)
    y = (xf - mu) * lax.rsqrt(var + LN_EPS)
    return (y * g.astype(jnp.float32) + b.astype(jnp.float32)).astype(x.dtype)


def pool_mixer(x, w_in, w_grp, scale, w_out):
    B, S, D = x.shape
    h = x @ w_in
    cs = jnp.cumsum(h.astype(jnp.float32), axis=1)
    cs = jnp.pad(cs, ((0, 0), (1, 0), (0, 0)))
    t = jnp.arange(S)
    outs = []
    for g, w in enumerate(POOL_WINDOWS):
        lo = jnp.maximum(t - w // 2, 0)
        hi = jnp.minimum(t + w // 2 - 1, S - 1)
        sl = slice(g * GROUP_DIM, (g + 1) * GROUP_DIM)
        csg = cs[..., sl]
        cnt = (hi - lo + 1).astype(jnp.float32)[:, None]
        mean = (csg[:, hi + 1] - csg[:, lo]) / cnt
        outs.append(mean - h[..., sl].astype(jnp.float32))
    pooled = jnp.stack(outs, axis=2).astype(x.dtype)
    mixed = jnp.einsum('bsgc,gcd->bsgd', pooled, w_grp).reshape(B, S, D) * scale
    return mixed @ w_out


def conv_mixer(x, w_in, w_conv, w_out):
    bg, cg, h = jnp.split(x @ w_in, 3, axis=-1)
    z = cg * h
    zp = jnp.pad(z, ((0, 0), (1, 1), (0, 0)))
    conv = zp[:, :-2] * w_conv[0] + zp[:, 1:-1] * w_conv[1] + zp[:, 2:] * w_conv[2]
    return (bg * conv) @ w_out


def gmlp_mixer(x, w_in, ln_g, ln_b, w_s, b_s, w_out):
    B, S, _ = x.shape
    u, v = jnp.split(jax.nn.gelu(x @ w_in, approximate=False), 2, axis=-1)
    v = layer_norm(v, ln_g, ln_b)
    vc = v.reshape(B, S // CHUNK, CHUNK, N_GROUPS, GMLP_GROUP_DIM)
    vs = jnp.einsum('gpq,bnqgc->bnpgc', w_s, vc) + b_s.T[:, :, None]
    return (u * vs.reshape(B, S, GMLP_HALF)) @ w_out


def fnet_mixer(x, w_in, ln_g, ln_b, w_out):
    B, S, D = x.shape
    h = (x @ w_in).reshape(B, S, N_GROUPS, GROUP_DIM)
    h = layer_norm(h, ln_g.reshape(N_GROUPS, GROUP_DIM), ln_b.reshape(N_GROUPS, GROUP_DIM))
    f = jnp.fft.fft2(h.astype(jnp.float32), axes=(1, 3), norm='ortho').real.astype(x.dtype)
    return f.reshape(B, S, D) @ w_out


def moe(x, router_w, router_b, w_gu, b_gu, w_down, b_down):
    B, S, D = x.shape
    N = B * S
    xt = x.reshape(N, D)
    logits = xt.astype(jnp.float32) @ router_w.astype(jnp.float32) + router_b.astype(jnp.float32)
    top_vals, top_idx = lax.top_k(logits, TOP_K)
    gates = jax.nn.softmax(top_vals, axis=-1).astype(x.dtype)
    A = N * TOP_K
    flat_e = top_idx.reshape(-1).astype(jnp.int32)
    flat_tok = jnp.repeat(jnp.arange(N, dtype=jnp.int32), TOP_K)
    flat_gate = gates.reshape(-1)
    order = jnp.argsort(flat_e, stable=True)
    sorted_e = flat_e[order]
    counts = jnp.bincount(flat_e, length=N_EXPERTS)
    starts = jnp.cumsum(counts) - counts
    padded = (counts + MOE_BLOCK - 1) // MOE_BLOCK * MOE_BLOCK
    pad_ends = jnp.cumsum(padded)
    pad_starts = pad_ends - padded
    dest = pad_starts[sorted_e] + jnp.arange(A, dtype=jnp.int32) - starts[sorted_e]
    n_blocks = -(-A // MOE_BLOCK) + N_EXPERTS
    P = n_blocks * MOE_BLOCK
    slot_tok = jnp.full((P,), N, jnp.int32).at[dest].set(flat_tok[order])
    slot_gate = jnp.zeros((P,), x.dtype).at[dest].set(flat_gate[order])
    block_e = jnp.minimum(
        jnp.searchsorted(pad_ends, jnp.arange(n_blocks, dtype=jnp.int32) * MOE_BLOCK, side='right'),
        N_EXPERTS - 1)
    x_pad = jnp.concatenate([xt, jnp.zeros((1, D), x.dtype)], axis=0)
    xs = x_pad[slot_tok].reshape(n_blocks, MOE_BLOCK, D)

    def expert_block(args):
        xb, e = args
        hgu = xb @ w_gu[e] + b_gu[e]
        gate, up = jnp.split(hgu, 2, axis=-1)
        gate = jnp.minimum(gate, SWIGLU_LIMIT)
        up = jnp.clip(up, -SWIGLU_LIMIT, SWIGLU_LIMIT)
        act = gate * jax.nn.sigmoid(SWIGLU_ALPHA * gate) * (up + 1)
        return act @ w_down[e] + b_down[e]

    ys = lax.map(expert_block, (xs, block_e))
    out = jnp.zeros((N + 1, D), x.dtype).at[slot_tok].add(ys.reshape(P, D) * slot_gate[:, None])
    return out[:N].reshape(B, S, D)


def trunk(x, pool_p, conv_p, gmlp_p, fnet_p, norm_p, moe_p):
    ln1_g, ln1_b, ln2_g, ln2_b = norm_p
    router_w, router_b, w_gu, b_gu, w_down, b_down = moe_p
    for i in range(DEPTH):
        m, j = i % N_MIXERS, i // N_MIXERS
        if m == 0:
            t = pool_mixer(x, *[p[j] for p in pool_p])
        elif m == 1:
            t = conv_mixer(x, *[p[j] for p in conv_p])
        elif m == 2:
            t = gmlp_mixer(x, *[p[j] for p in gmlp_p])
        else:
            t = fnet_mixer(x, *[p[j] for p in fnet_p])
        x = layer_norm(DEEPNORM_ALPHA * x + t, ln1_g[i], ln1_b[i])
        f = moe(x, router_w[i], router_b[i], w_gu[i], b_gu[i], w_down[i], b_down[i])
        x = layer_norm(DEEPNORM_ALPHA * x + f, ln2_g[i], ln2_b[i])
    return x


def setup_inputs(seed: int = 0) -> dict:
    key = jax.random.key(seed)
    ks = iter(jax.random.split(key, 40))
    D = D_MODEL

    def nrm(shape, scale):
        return jax.random.normal(next(ks), shape, jnp.float32) * scale

    nA, nB, nC, nD = (_n_layers_of(m) for m in range(N_MIXERS))
    beta_out = DEEPNORM_BETA * D ** -0.5
    return {
        'x_prompt': nrm((BATCH, SEQ, D), 1.0),
        'x_sample': nrm((DEC_BATCH, DEC_SEQ, D), 1.0),
        'pool_w_in': nrm((nA, D, D), D ** -0.5),
        'pool_w_grp': nrm((nA, N_GROUPS, GROUP_DIM, GROUP_DIM), GROUP_DIM ** -0.5),
        'pool_scale': 1.0 + nrm((nA, D), 0.02),
        'pool_w_out': nrm((nA, D, D), beta_out),
        'conv_w_in': nrm((nB, D, 3 * D), D ** -0.5),
        'conv_w': nrm((nB, CONV_WIDTH, D), CONV_WIDTH ** -0.5),
        'conv_w_out': nrm((nB, D, D), beta_out),
        'gmlp_w_in': nrm((nC, D, 2 * GMLP_HALF), D ** -0.5),
        'gmlp_ln_g': 1.0 + nrm((nC, GMLP_HALF), 0.02),
        'gmlp_ln_b': nrm((nC, GMLP_HALF), 0.02),
        'gmlp_w_s': nrm((nC, N_GROUPS, CHUNK, CHUNK), CHUNK ** -0.5),
        'gmlp_b_s': 1.0 + nrm((nC, N_GROUPS, CHUNK), 0.02),
        'gmlp_w_out': nrm((nC, GMLP_HALF, D), DEEPNORM_BETA * GMLP_HALF ** -0.5),
        'fnet_w_in': nrm((nD, D, D), D ** -0.5),
        'fnet_ln_g': 1.0 + nrm((nD, D), 0.02),
        'fnet_ln_b': nrm((nD, D), 0.02),
        'fnet_w_out': nrm((nD, D, D), beta_out),
        'ln1_g': 1.0 + nrm((DEPTH, D), 0.02),
        'ln1_b': nrm((DEPTH, D), 0.02),
        'ln2_g': 1.0 + nrm((DEPTH, D), 0.02),
        'ln2_b': nrm((DEPTH, D), 0.02),
        'router_w': nrm((DEPTH, D, N_EXPERTS), D ** -0.5),
        'router_b': nrm((DEPTH, N_EXPERTS), 0.01),
        'moe_w_gu': nrm((DEPTH, N_EXPERTS, D, 2 * D_EXPERT), D ** -0.5),
        'moe_b_gu': nrm((DEPTH, N_EXPERTS, 2 * D_EXPERT), 0.02),
        'moe_w_down': nrm((DEPTH, N_EXPERTS, D_EXPERT, D), DEEPNORM_BETA * D_EXPERT ** -0.5),
        'moe_b_down': nrm((DEPTH, N_EXPERTS, D), 0.02),
    }


def reference(x_prompt, x_sample, pool_w_in, pool_w_grp, pool_scale, pool_w_out,
              conv_w_in, conv_w, conv_w_out,
              gmlp_w_in, gmlp_ln_g, gmlp_ln_b, gmlp_w_s, gmlp_b_s, gmlp_w_out,
              fnet_w_in, fnet_ln_g, fnet_ln_b, fnet_w_out,
              ln1_g, ln1_b, ln2_g, ln2_b,
              router_w, router_b, moe_w_gu, moe_b_gu, moe_w_down, moe_b_down):
    pool_p = (pool_w_in, pool_w_grp, pool_scale, pool_w_out)
    conv_p = (conv_w_in, conv_w, conv_w_out)
    gmlp_p = (gmlp_w_in, gmlp_ln_g, gmlp_ln_b, gmlp_w_s, gmlp_b_s, gmlp_w_out)
    fnet_p = (fnet_w_in, fnet_ln_g, fnet_ln_b, fnet_w_out)
    norm_p = (ln1_g, ln1_b, ln2_g, ln2_b)
    moe_p = (router_w, router_b, moe_w_gu, moe_b_gu, moe_w_down, moe_b_down)
    y_prompt = trunk(x_prompt, pool_p, conv_p, gmlp_p, fnet_p, norm_p, moe_p)
    y_sample = trunk(x_sample, pool_p, conv_p, gmlp_p, fnet_p, norm_p, moe_p)
    return (y_prompt, y_sample)
```

```python
import functools
import math

import jax
import jax.numpy as jnp
from jax import lax
from jax.experimental import pallas as pl
from jax.experimental.pallas import tpu as pltpu

D_MODEL = 1024
DEPTH = 4
N_GROUPS = 4
GROUP_DIM = D_MODEL // N_GROUPS
POOL_WINDOWS = (2, 4, 8, 16)
CHUNK = 128
N_EXPERTS = 32
TOP_K = 4
SWIGLU_ALPHA = 1.702
SWIGLU_LIMIT = 7.0
LN_EPS = 1e-5
DEEPNORM_ALPHA = (2 * DEPTH) ** 0.25

LANES = 128
SUBLANES = 8
HALO = SUBLANES
TILE_ROWS = 512
MOE_BLOCK = 512
DFT_TM = 512
DFT_TK = 512
VMEM_LIMIT = 56 * 1024 * 1024
NEG_BIG = -1e30

F32 = jnp.float32
BF16 = jnp.bfloat16


def _ln(v, g, b):
    mu = jnp.mean(v, axis=-1, keepdims=True)
    d = v - mu
    var = jnp.mean(d * d, axis=-1, keepdims=True)
    return d * lax.rsqrt(var + LN_EPS) * g + b


def _dot(a, b):
    return jnp.dot(a, b, preferred_element_type=F32)


def _seq_pos(i, ts, n_prompt, s1, s2):
    row0 = i * ts
    in_prompt = row0 < n_prompt
    s = jnp.where(in_prompt, s1, s2)
    pos0 = jnp.where(in_prompt, lax.rem(row0, s1), lax.rem(row0 - n_prompt, s2))
    return s, pos0


def _ext_rows(xp_ref, xc, xn_ref):
    return jnp.concatenate([xp_ref[...], xc, xn_ref[...]], axis=0).astype(BF16)


def _ext_positions(pos0, n_ext):
    return lax.broadcasted_iota(jnp.int32, (n_ext, 1), 0) + (pos0 - HALO)


def _pool_kernel(xp_ref, xc_ref, xn_ref, win_ref, wgrp_ref, scale_ref, wout_ref,
                 g_ref, b_ref, o_ref, *, ts, n_prompt, s1, s2):
    s, pos0 = _seq_pos(pl.program_id(0), ts, n_prompt, s1, s2)
    n_ext = ts + 2 * HALO
    xc = xc_ref[...]
    h = _dot(_ext_rows(xp_ref, xc, xn_ref), win_ref[...])
    r = _ext_positions(pos0, n_ext)
    h = jnp.where((r >= 0) & (r < s), h, 0.0)
    pos = r[HALO:HALO + ts]
    outs = []
    for g, w in enumerate(POOL_WINDOWS):
        hg = h[:, g * GROUP_DIM:(g + 1) * GROUP_DIM]
        acc = hg + pltpu.roll(hg, 1, 0)
        step = 1
        while 2 * step < w:
            acc = pltpu.roll(acc, step, 0) + pltpu.roll(acc, n_ext - step, 0)
            step *= 2
        lo = jnp.maximum(pos - w // 2, 0)
        hi = jnp.minimum(pos + w // 2 - 1, s - 1)
        cnt = (hi - lo + 1).astype(F32)
        pooled = acc[HALO:HALO + ts] / cnt - hg[HALO:HALO + ts]
        outs.append(_dot(pooled.astype(BF16), wgrp_ref[g]))
    mixed = jnp.concatenate(outs, axis=1) * scale_ref[...]
    t = _dot(mixed.astype(BF16), wout_ref[...])
    o_ref[...] = _ln(DEEPNORM_ALPHA * xc + t, g_ref[...], b_ref[...])


def _conv_kernel(xp_ref, xc_ref, xn_ref, win_ref, wconv_ref, wout_ref,
                 g_ref, b_ref, o_ref, *, ts, n_prompt, s1, s2):
    d = D_MODEL
    s, pos0 = _seq_pos(pl.program_id(0), ts, n_prompt, s1, s2)
    n_ext = ts + 2 * HALO
    xc = xc_ref[...]
    xe = _ext_rows(xp_ref, xc, xn_ref)
    r = _ext_positions(pos0, n_ext)
    cg = _dot(xe, win_ref[:, d:2 * d])
    hh = _dot(xe, win_ref[:, 2 * d:3 * d])
    z = jnp.where((r >= 0) & (r < s), cg * hh, 0.0)
    c = slice(HALO, HALO + ts)
    conv = (pltpu.roll(z, 1, 0)[c] * wconv_ref[0:1, :] + z[c] * wconv_ref[1:2, :]
            + pltpu.roll(z, n_ext - 1, 0)[c] * wconv_ref[2:3, :])
    bg = _dot(xc.astype(BF16), win_ref[:, 0:d])
    t = _dot((bg * conv).astype(BF16), wout_ref[...])
    o_ref[...] = _ln(DEEPNORM_ALPHA * xc + t, g_ref[...], b_ref[...])


def _gmlp_kernel(xc_ref, win_ref, lng_ref, lnb_ref, ws_ref, bs_ref, wout_ref,
                 g_ref, b_ref, o_ref, *, ts):
    d = D_MODEL
    xc = xc_ref[...]
    xb = xc.astype(BF16)

    def gelu(a):
        return 0.5 * a * (1.0 + lax.erf(a * (1.0 / math.sqrt(2.0))))

    u = gelu(_dot(xb, win_ref[:, 0:d]))
    v = gelu(_dot(xb, win_ref[:, d:2 * d]))
    v = _ln(v, lng_ref[...], lnb_ref[...]).astype(BF16)
    cols = []
    for g in range(N_GROUPS):
        w_s = ws_ref[g]
        b_s = bs_ref[:, g:g + 1]
        rows = []
        for c in range(ts // CHUNK):
            vc = v[c * CHUNK:(c + 1) * CHUNK, g * GROUP_DIM:(g + 1) * GROUP_DIM]
            rows.append(_dot(w_s, vc) + b_s)
        cols.append(jnp.concatenate(rows, axis=0))
    vs = jnp.concatenate(cols, axis=1)
    t = _dot((u * vs).astype(BF16), wout_ref[...])
    o_ref[...] = _ln(DEEPNORM_ALPHA * xc + t, g_ref[...], b_ref[...])


def _fnet_in_kernel(xc_ref, win_ref, lng_ref, lnb_ref, cc_ref, sc_ref, hc_ref, hs_ref):
    h = _dot(xc_ref[...].astype(BF16), win_ref[...])
    for g in range(N_GROUPS):
        sl = slice(g * GROUP_DIM, (g + 1) * GROUP_DIM)
        hn = _ln(h[:, sl], lng_ref[:, sl], lnb_ref[:, sl]).astype(BF16)
        hc_ref[:, sl] = _dot(hn, cc_ref[...]).astype(BF16)
        hs_ref[:, sl] = _dot(hn, sc_ref[...]).astype(BF16)


def _fnet_seq_kernel(cs_ref, sn_ref, hc_ref, hs_ref, x_ref, wout_ref, g_ref, b_ref,
                     o_ref, acc_ref, *, scale):
    k = pl.program_id(2)

    @pl.when(k == 0)
    def _():
        acc_ref[...] = jnp.zeros_like(acc_ref)

    acc_ref[...] += _dot(cs_ref[...], hc_ref[...]) + _dot(sn_ref[...], hs_ref[...])

    @pl.when(k == pl.num_programs(2) - 1)
    def _():
        f = acc_ref[...] * scale
        t = _dot(f.astype(BF16), wout_ref[...])
        o_ref[...] = _ln(DEEPNORM_ALPHA * x_ref[...] + t, g_ref[...], b_ref[...])


def _router_kernel(x_ref, rw_ref, rb_ref, meta_t_ref, meta_ref, counts_ref, carry_ref, *, ts):
    i = pl.program_id(0)

    @pl.when(i == 0)
    def _():
        carry_ref[...] = jnp.zeros_like(carry_ref)

    logits = jnp.dot(x_ref[...], rw_ref[...], precision=lax.Precision.HIGHEST,
                     preferred_element_type=F32) + rb_ref[...]
    lane = lax.broadcasted_iota(jnp.int32, (ts, LANES), 1)
    work = logits
    vals, idxs, sels = [], [], []
    for _ in range(TOP_K):
        m = jnp.max(work, axis=-1, keepdims=True)
        idx = jnp.min(jnp.where(work == m, lane, LANES), axis=-1, keepdims=True)
        sel = lane == idx
        vals.append(m)
        idxs.append(idx)
        sels.append(sel)
        work = jnp.where(sel, -jnp.inf, work)
    exps = [jnp.exp(v - vals[0]) for v in vals]
    denom = exps[0] + exps[1] + exps[2] + exps[3]
    selmask = jnp.where(sels[0] | sels[1] | sels[2] | sels[3], 1.0, 0.0)
    tri = (lax.broadcasted_iota(jnp.int32, (ts, ts), 0)
           > lax.broadcasted_iota(jnp.int32, (ts, ts), 1)).astype(BF16)
    rank = _dot(tri, selmask.astype(BF16)) + carry_ref[...]
    carry_ref[...] += jnp.sum(selmask, axis=0, keepdims=True)
    meta = jnp.zeros((ts, LANES), F32)
    for k in range(TOP_K):
        rank_k = jnp.sum(jnp.where(sels[k], rank, 0.0), axis=-1, keepdims=True)
        meta = jnp.where(lane == k, idxs[k].astype(F32), meta)
        meta = jnp.where(lane == TOP_K + k, rank_k, meta)
        meta = jnp.where(lane == 2 * TOP_K + k, exps[k] / denom, meta)
    meta_ref[...] = meta
    meta_t_ref[...] = meta.T[0:2 * SUBLANES, :]
    counts_ref[...] = carry_ref[...]


def _dispatch_kernel(dest_ref, x_ref, xs_in_ref, xs_ref, sem, *, ts):
    del xs_in_ref

    def row_copy(t, k):
        return pltpu.make_async_copy(
            x_ref.at[pl.ds(t, 1), :], xs_ref.at[pl.ds(dest_ref[k, t], 1), :], sem)

    def issue(t, carry):
        for k in range(TOP_K):
            row_copy(t, k).start()
        return carry

    def drain(t, carry):
        for k in range(TOP_K):
            row_copy(t, k).wait()
        return carry

    lax.fori_loop(0, ts, issue, 0)
    lax.fori_loop(0, ts, drain, 0)


def _expert_kernel(be_ref, nused_ref, xs_ref, wgu_ref, bgu_ref, wd_ref, bd_ref, ys_ref):
    d = D_MODEL

    @pl.when(pl.program_id(0) < nused_ref[0])
    def _():
        xb = xs_ref[...].astype(BF16)
        gate = _dot(xb, wgu_ref[:, 0:d]) + bgu_ref[:, 0:d]
        up = _dot(xb, wgu_ref[:, d:2 * d]) + bgu_ref[:, d:2 * d]
        gate = jnp.minimum(gate, SWIGLU_LIMIT)
        up = jnp.clip(up, -SWIGLU_LIMIT, SWIGLU_LIMIT)
        act = gate * jax.nn.sigmoid(SWIGLU_ALPHA * gate) * (up + 1.0)
        ys_ref[...] = _dot(act.astype(BF16), wd_ref[...]) + bd_ref[...]


def _combine_kernel(dest_ref, x_ref, meta_ref, g_ref, b_ref, ys_ref, o_ref, ybuf, sem, *, ts):
    def row_copy(t, k):
        return pltpu.make_async_copy(
            ys_ref.at[pl.ds(dest_ref[k, t], 1), :], ybuf.at[k, pl.ds(t, 1), :], sem)

    def issue(t, carry):
        for k in range(TOP_K):
            row_copy(t, k).start()
        return carry

    def drain(t, carry):
        for k in range(TOP_K):
            row_copy(t, k).wait()
        return carry

    lax.fori_loop(0, ts, issue, 0)
    lax.fori_loop(0, ts, drain, 0)
    acc = DEEPNORM_ALPHA * x_ref[...]
    for k in range(TOP_K):
        acc = acc + meta_ref[:, 2 * TOP_K + k:2 * TOP_K + k + 1] * ybuf[k]
    o_ref[...] = _ln(acc, g_ref[...], b_ref[...])


def _params(sem):
    return pltpu.CompilerParams(dimension_semantics=sem, vmem_limit_bytes=VMEM_LIMIT)


def _const(shape):
    return pl.BlockSpec(shape, lambda *_: (0,) * len(shape))


def _row(v):
    return v.reshape(1, -1).astype(F32)


def _tile_specs(ts, nt):
    hb = ts // HALO
    last = nt // HALO - 1
    prev = pl.BlockSpec((HALO, D_MODEL), lambda i: (jnp.maximum(i * hb - 1, 0), 0))
    cur = pl.BlockSpec((ts, D_MODEL), lambda i: (i, 0))
    nxt = pl.BlockSpec((HALO, D_MODEL), lambda i: (jnp.minimum((i + 1) * hb, last), 0))
    return prev, cur, nxt


def _pool_layer(x, w_in, w_grp, scale, w_out, g, b, *, ts, n_prompt, s1, s2):
    nt, d = x.shape
    prev, cur, nxt = _tile_specs(ts, nt)
    return pl.pallas_call(
        functools.partial(_pool_kernel, ts=ts, n_prompt=n_prompt, s1=s1, s2=s2),
        out_shape=jax.ShapeDtypeStruct((nt, d), F32),
        grid=(nt // ts,),
        in_specs=[prev, cur, nxt, _const((d, d)), _const((N_GROUPS, GROUP_DIM, GROUP_DIM)),
                  _const((1, d)), _const((d, d)), _const((1, d)), _const((1, d))],
        out_specs=cur,
        compiler_params=_params(("parallel",)),
        name="pool_layer",
    )(x, x, x, w_in.astype(BF16), w_grp.astype(BF16), _row(scale), w_out.astype(BF16), _row(g), _row(b))


def _conv_layer(x, w_in, w_conv, w_out, g, b, *, ts, n_prompt, s1, s2):
    nt, d = x.shape
    prev, cur, nxt = _tile_specs(ts, nt)
    return pl.pallas_call(
        functools.partial(_conv_kernel, ts=ts, n_prompt=n_prompt, s1=s1, s2=s2),
        out_shape=jax.ShapeDtypeStruct((nt, d), F32),
        grid=(nt // ts,),
        in_specs=[prev, cur, nxt, _const((d, 3 * d)), _const((3, d)), _const((d, d)),
                  _const((1, d)), _const((1, d))],
        out_specs=cur,
        compiler_params=_params(("parallel",)),
        name="conv_layer",
    )(x, x, x, w_in.astype(BF16), w_conv.astype(F32), w_out.astype(BF16), _row(g), _row(b))


def _gmlp_layer(x, w_in, ln_g, ln_b, w_s, b_s, w_out, g, b, *, ts):
    nt, d = x.shape
    cur = pl.BlockSpec((ts, d), lambda i: (i, 0))
    return pl.pallas_call(
        functools.partial(_gmlp_kernel, ts=ts),
        out_shape=jax.ShapeDtypeStruct((nt, d), F32),
        grid=(nt // ts,),
        in_specs=[cur, _const((d, 2 * d)), _const((1, d)), _const((1, d)),
                  _const((N_GROUPS, CHUNK, CHUNK)), _const((CHUNK, N_GROUPS)), _const((d, d)),
                  _const((1, d)), _const((1, d))],
        out_specs=cur,
        compiler_params=_params(("parallel",)),
        name="gmlp_layer",
    )(x, w_in.astype(BF16), _row(ln_g), _row(ln_b), w_s.astype(BF16), b_s.T.astype(F32),
      w_out.astype(BF16), _row(g), _row(b))


def _dft_tables(n):
    j = lax.broadcasted_iota(jnp.int32, (n, n), 0)
    k = lax.broadcasted_iota(jnp.int32, (n, n), 1)
    ang = (2.0 * math.pi / n) * lax.rem(j * k, n).astype(F32)
    return jnp.cos(ang), -jnp.sin(ang)


def _fnet_layer(x, w_in, ln_g, ln_b, w_out, g, b, *, ts, segments):
    nt, d = x.shape
    cur = pl.BlockSpec((ts, d), lambda i: (i, 0))
    cc, sn_c = _dft_tables(GROUP_DIM)
    hc, hs = pl.pallas_call(
        _fnet_in_kernel,
        out_shape=(jax.ShapeDtypeStruct((nt, d), BF16), jax.ShapeDtypeStruct((nt, d), BF16)),
        grid=(nt // ts,),
        in_specs=[cur, _const((d, d)), _const((1, d)), _const((1, d)),
                  _const((GROUP_DIM, GROUP_DIM)), _const((GROUP_DIM, GROUP_DIM))],
        out_specs=(cur, cur),
        compiler_params=_params(("parallel",)),
        name="fnet_in",
    )(x, w_in.astype(BF16), _row(ln_g), _row(ln_b), cc.astype(BF16), (-sn_c).astype(BF16))
    outs = []
    for base, nb, s in segments:
        tm, tk = min(DFT_TM, s), min(DFT_TK, s)
        cs, sn = _dft_tables(s)
        mb, kb = base // tm, base // tk
        row_m = pl.BlockSpec((tm, d), lambda bi, m, k, mb=mb, s=s, tm=tm: (mb + bi * (s // tm) + m, 0))
        row_k = pl.BlockSpec((tk, d), lambda bi, m, k, kb=kb, s=s, tk=tk: (kb + bi * (s // tk) + k, 0))
        tab = pl.BlockSpec((tm, tk), lambda bi, m, k: (m, k))
        outs.append(pl.pallas_call(
            functools.partial(_fnet_seq_kernel, scale=1.0 / math.sqrt(s * GROUP_DIM)),
            out_shape=jax.ShapeDtypeStruct((nb * s, d), F32),
            grid=(nb, s // tm, s // tk),
            in_specs=[tab, tab, row_k, row_k, row_m, _const((d, d)), _const((1, d)), _const((1, d))],
            out_specs=pl.BlockSpec((tm, d), lambda bi, m, k, s=s, tm=tm: (bi * (s // tm) + m, 0)),
            scratch_shapes=[pltpu.VMEM((tm, d), F32)],
            compiler_params=_params(("parallel", "parallel", "arbitrary")),
            name="fnet_seq",
        )(cs.astype(BF16), sn.astype(BF16), hc, hs, x, w_out.astype(BF16), _row(g), _row(b)))
    return jnp.concatenate(outs, axis=0)


def _moe_layer(x, layer, router_w, router_b, w_gu, b_gu, w_down, b_down, g, b, *, ts):
    nt, d = x.shape
    n_tiles = nt // ts
    cur = pl.BlockSpec((ts, d), lambda i: (i, 0))
    rw = jnp.zeros((d, LANES), F32).at[:, :N_EXPERTS].set(router_w.astype(F32))
    rb = jnp.full((1, LANES), NEG_BIG, F32).at[0, :N_EXPERTS].set(router_b.astype(F32))
    meta_t, meta, counts = pl.pallas_call(
        functools.partial(_router_kernel, ts=ts),
        out_shape=(jax.ShapeDtypeStruct((n_tiles, 2 * SUBLANES, ts), F32),
                   jax.ShapeDtypeStruct((nt, LANES), F32),
                   jax.ShapeDtypeStruct((1, LANES), F32)),
        grid=(n_tiles,),
        in_specs=[cur, _const((d, LANES)), _const((1, LANES))],
        out_specs=(pl.BlockSpec((None, 2 * SUBLANES, ts), lambda i: (i, 0, 0)),
                   pl.BlockSpec((ts, LANES), lambda i: (i, 0)),
                   _const((1, LANES))),
        scratch_shapes=[pltpu.VMEM((1, LANES), F32)],
        compiler_params=_params(("arbitrary",)),
        name="moe_router",
    )(x, rw, rb)

    n_blocks = (nt * TOP_K) // MOE_BLOCK + N_EXPERTS
    n_slots = n_blocks * MOE_BLOCK
    cnt = counts[0, :N_EXPERTS].astype(jnp.int32)
    padded = (cnt + MOE_BLOCK - 1) // MOE_BLOCK * MOE_BLOCK
    pad_ends = jnp.cumsum(padded)
    pad_starts = pad_ends - padded
    block_e = jnp.minimum(
        jnp.searchsorted(pad_ends, jnp.arange(n_blocks, dtype=jnp.int32) * MOE_BLOCK, side='right'),
        N_EXPERTS - 1).astype(jnp.int32)
    n_used = (pad_ends[-1:] // MOE_BLOCK).astype(jnp.int32)
    e_idx = meta_t[:, 0:TOP_K, :].astype(jnp.int32)
    rank = meta_t[:, TOP_K:2 * TOP_K, :].astype(jnp.int32)
    dest = rank
    for e in range(N_EXPERTS):
        dest = dest + jnp.where(e_idx == e, pad_starts[e], 0)

    dest_spec = pl.BlockSpec((None, TOP_K, ts), lambda i: (i, 0, 0), memory_space=pltpu.SMEM)
    any_spec = pl.BlockSpec(memory_space=pl.ANY)
    xs = pl.pallas_call(
        functools.partial(_dispatch_kernel, ts=ts),
        out_shape=jax.ShapeDtypeStruct((n_slots, d), F32),
        grid=(n_tiles,),
        in_specs=[dest_spec, cur, any_spec],
        out_specs=any_spec,
        scratch_shapes=[pltpu.SemaphoreType.DMA],
        input_output_aliases={2: 0},
        compiler_params=_params(("arbitrary",)),
        name="moe_dispatch",
    )(dest, x, jnp.zeros((n_slots, d), F32))

    blk = pl.BlockSpec((MOE_BLOCK, d), lambda j, be, nu: (j, 0))
    ys = pl.pallas_call(
        _expert_kernel,
        out_shape=jax.ShapeDtypeStruct((n_slots, d), F32),
        grid_spec=pltpu.PrefetchScalarGridSpec(
            num_scalar_prefetch=2,
            grid=(n_blocks,),
            in_specs=[blk,
                      pl.BlockSpec((None, None, d, 2 * d), lambda j, be, nu: (layer, be[j], 0, 0)),
                      pl.BlockSpec((None, None, 1, 2 * d), lambda j, be, nu: (layer, be[j], 0, 0)),
                      pl.BlockSpec((None, None, d, d), lambda j, be, nu: (layer, be[j], 0, 0)),
                      pl.BlockSpec((None, None, 1, d), lambda j, be, nu: (layer, be[j], 0, 0))],
            out_specs=blk),
        compiler_params=_params(("arbitrary",)),
        name="moe_experts",
    )(block_e, n_used, xs, w_gu, b_gu, w_down, b_down)

    return pl.pallas_call(
        functools.partial(_combine_kernel, ts=ts),
        out_shape=jax.ShapeDtypeStruct((nt, d), F32),
        grid=(n_tiles,),
        in_specs=[dest_spec, cur, pl.BlockSpec((ts, LANES), lambda i: (i, 0)),
                  _const((1, d)), _const((1, d)), any_spec],
        out_specs=cur,
        scratch_shapes=[pltpu.VMEM((TOP_K, ts, d), F32), pltpu.SemaphoreType.DMA],
        compiler_params=_params(("arbitrary",)),
        name="moe_combine",
    )(dest, x, meta, _row(g), _row(b), ys)


def kernel(x_prompt, x_sample, pool_w_in, pool_w_grp, pool_scale, pool_w_out, conv_w_in, conv_w, conv_w_out, gmlp_w_in, gmlp_ln_g, gmlp_ln_b, gmlp_w_s, gmlp_b_s, gmlp_w_out, fnet_w_in, fnet_ln_g, fnet_ln_b, fnet_w_out, ln1_g, ln1_b, ln2_g, ln2_b, router_w, router_b, moe_w_gu, moe_b_gu, moe_w_down, moe_b_down):
    b1, s1, d = x_prompt.shape
    b2, s2, _ = x_sample.shape
    assert d == D_MODEL and ln1_g.shape[0] == DEPTH
    n_prompt = b1 * s1
    ts = math.gcd(TILE_ROWS, math.gcd(s1, s2))
    assert ts % CHUNK == 0
    x = jnp.concatenate([x_prompt.reshape(n_prompt, d), x_sample.reshape(b2 * s2, d)], axis=0)
    seq = dict(ts=ts, n_prompt=n_prompt, s1=s1, s2=s2)
    segments = ((0, b1, s1), (n_prompt, b2, s2))
    w_gu = moe_w_gu.astype(BF16)
    w_down = moe_w_down.astype(BF16)
    b_gu = moe_b_gu.astype(F32)[:, :, None, :]
    b_down = moe_b_down.astype(F32)[:, :, None, :]
    for i in range(DEPTH):
        m, j = i % 4, i // 4
        if m == 0:
            x = _pool_layer(x, pool_w_in[j], pool_w_grp[j], pool_scale[j], pool_w_out[j],
                            ln1_g[i], ln1_b[i], **seq)
        elif m == 1:
            x = _conv_layer(x, conv_w_in[j], conv_w[j], conv_w_out[j], ln1_g[i], ln1_b[i], **seq)
        elif m == 2:
            x = _gmlp_layer(x, gmlp_w_in[j], gmlp_ln_g[j], gmlp_ln_b[j], gmlp_w_s[j], gmlp_b_s[j],
                            gmlp_w_out[j], ln1_g[i], ln1_b[i], ts=ts)
        else:
            x = _fnet_layer(x, fnet_w_in[j], fnet_ln_g[j], fnet_ln_b[j], fnet_w_out[j],
                            ln1_g[i], ln1_b[i], ts=ts, segments=segments)
        x = _moe_layer(x, i, router_w[i], router_b[i], w_gu, b_gu, w_down, b_down,
                       ln2_g[i], ln2_b[i], ts=ts)
    return (x[:n_prompt].reshape(b1, s1, d), x[n_prompt:].reshape(b2, s2, d))
```

```python
import functools
import math

import jax
import jax.numpy as jnp
from jax import lax
from jax.experimental import pallas as pl
from jax.experimental.pallas import tpu as pltpu

D_MODEL = 1024
DEPTH = 4
N_GROUPS = 4
GROUP_DIM = D_MODEL // N_GROUPS
POOL_WINDOWS = (2, 4, 8, 16)
CHUNK = 128
N_EXPERTS = 32
TOP_K = 4
SWIGLU_ALPHA = 1.702
SWIGLU_LIMIT = 7.0
LN_EPS = 1e-5
DEEPNORM_ALPHA = (2 * DEPTH) ** 0.25

LANES = 128
SUBLANES = 8
HALO = SUBLANES
TILE_ROWS = 512
MOE_BLOCK = 512
DFT_TM = 512
DFT_TK = 512
VMEM_LIMIT = 56 * 1024 * 1024
NEG_BIG = -1e30

F32 = jnp.float32
BF16 = jnp.bfloat16


def _ln(v, g, b):
    mu = jnp.mean(v, axis=-1, keepdims=True)
    d = v - mu
    var = jnp.mean(d * d, axis=-1, keepdims=True)
    return d * lax.rsqrt(var + LN_EPS) * g + b


def _dot(a, b):
    return jnp.dot(a, b, preferred_element_type=F32)


def _seq_pos(i, ts, n_prompt, s1, s2):
    row0 = i * ts
    in_prompt = row0 < n_prompt
    s = jnp.where(in_prompt, s1, s2)
    pos0 = jnp.where(in_prompt, lax.rem(row0, s1), lax.rem(row0 - n_prompt, s2))
    return s, pos0


def _ext_rows(xp_ref, xc, xn_ref):
    return jnp.concatenate([xp_ref[...], xc, xn_ref[...]], axis=0).astype(BF16)


def _ext_positions(pos0, n_ext):
    return lax.broadcasted_iota(jnp.int32, (n_ext, 1), 0) + (pos0 - HALO)


def _pool_kernel(xp_ref, xc_ref, xn_ref, win_ref, wgrp_ref, scale_ref, wout_ref,
                 g_ref, b_ref, o_ref, *, ts, n_prompt, s1, s2):
    s, pos0 = _seq_pos(pl.program_id(0), ts, n_prompt, s1, s2)
    n_ext = ts + 2 * HALO
    xc = xc_ref[...]
    h = _dot(_ext_rows(xp_ref, xc, xn_ref), win_ref[...])
    r = _ext_positions(pos0, n_ext)
    h = jnp.where((r >= 0) & (r < s), h, 0.0)
    pos = r[HALO:HALO + ts]
    outs = []
    for g, w in enumerate(POOL_WINDOWS):
        hg = h[:, g * GROUP_DIM:(g + 1) * GROUP_DIM]
        acc = hg + pltpu.roll(hg, 1, 0)
        step = 1
        while 2 * step < w:
            acc = pltpu.roll(acc, step, 0) + pltpu.roll(acc, n_ext - step, 0)
            step *= 2
        lo = jnp.maximum(pos - w // 2, 0)
        hi = jnp.minimum(pos + w // 2 - 1, s - 1)
        cnt = (hi - lo + 1).astype(F32)
        pooled = acc[HALO:HALO + ts] / cnt - hg[HALO:HALO + ts]
        outs.append(_dot(pooled.astype(BF16), wgrp_ref[g]))
    mixed = jnp.concatenate(outs, axis=1) * scale_ref[...]
    t = _dot(mixed.astype(BF16), wout_ref[...])
    o_ref[...] = _ln(DEEPNORM_ALPHA * xc + t, g_ref[...], b_ref[...])


def _conv_kernel(xp_ref, xc_ref, xn_ref, win_ref, wconv_ref, wout_ref,
                 g_ref, b_ref, o_ref, *, ts, n_prompt, s1, s2):
    d = D_MODEL
    s, pos0 = _seq_pos(pl.program_id(0), ts, n_prompt, s1, s2)
    n_ext = ts + 2 * HALO
    xc = xc_ref[...]
    xe = _ext_rows(xp_ref, xc, xn_ref)
    r = _ext_positions(pos0, n_ext)
    cg = _dot(xe, win_ref[:, d:2 * d])
    hh = _dot(xe, win_ref[:, 2 * d:3 * d])
    z = jnp.where((r >= 0) & (r < s), cg * hh, 0.0)
    c = slice(HALO, HALO + ts)
    conv = (pltpu.roll(z, 1, 0)[c] * wconv_ref[0:1, :] + z[c] * wconv_ref[1:2, :]
            + pltpu.roll(z, n_ext - 1, 0)[c] * wconv_ref[2:3, :])
    bg = _dot(xc.astype(BF16), win_ref[:, 0:d])
    t = _dot((bg * conv).astype(BF16), wout_ref[...])
    o_ref[...] = _ln(DEEPNORM_ALPHA * xc + t, g_ref[...], b_ref[...])


def _gmlp_kernel(xc_ref, win_ref, lng_ref, lnb_ref, ws_ref, bs_ref, wout_ref,
                 g_ref, b_ref, o_ref, *, ts):
    d = D_MODEL
    xc = xc_ref[...]
    xb = xc.astype(BF16)

    def gelu(a):
        return 0.5 * a * (1.0 + lax.erf(a * (1.0 / math.sqrt(2.0))))

    u = gelu(_dot(xb, win_ref[:, 0:d]))
    v = gelu(_dot(xb, win_ref[:, d:2 * d]))
    v = _ln(v, lng_ref[...], lnb_ref[...]).astype(BF16)
    cols = []
    for g in range(N_GROUPS):
        w_s = ws_ref[g]
        b_s = bs_ref[:, g:g + 1]
        rows = []
        for c in range(ts // CHUNK):
            vc = v[c * CHUNK:(c + 1) * CHUNK, g * GROUP_DIM:(g + 1) * GROUP_DIM]
            rows.append(_dot(w_s, vc) + b_s)
        cols.append(jnp.concatenate(rows, axis=0))
    vs = jnp.concatenate(cols, axis=1)
    t = _dot((u * vs).astype(BF16), wout_ref[...])
    o_ref[...] = _ln(DEEPNORM_ALPHA * xc + t, g_ref[...], b_ref[...])


def _fnet_in_kernel(xc_ref, win_ref, lng_ref, lnb_ref, cc_ref, sc_ref, hc_ref, hs_ref):
    h = _dot(xc_ref[...].astype(BF16), win_ref[...])
    for g in range(N_GROUPS):
        sl = slice(g * GROUP_DIM, (g + 1) * GROUP_DIM)
        hn = _ln(h[:, sl], lng_ref[:, sl], lnb_ref[:, sl]).astype(BF16)
        hc_ref[:, sl] = _dot(hn, cc_ref[...]).astype(BF16)
        hs_ref[:, sl] = _dot(hn, sc_ref[...]).astype(BF16)


def _fnet_seq_kernel(cs_ref, sn_ref, hc_ref, hs_ref, x_ref, wout_ref, g_ref, b_ref,
                     o_ref, acc_ref, *, scale):
    k = pl.program_id(2)

    @pl.when(k == 0)
    def _():
        acc_ref[...] = jnp.zeros_like(acc_ref)

    acc_ref[...] += _dot(cs_ref[...], hc_ref[...]) + _dot(sn_ref[...], hs_ref[...])

    @pl.when(k == pl.num_programs(2) - 1)
    def _():
        f = acc_ref[...] * scale
        t = _dot(f.astype(BF16), wout_ref[...])
        o_ref[...] = _ln(DEEPNORM_ALPHA * x_ref[...] + t, g_ref[...], b_ref[...])


def _router_kernel(x_ref, rw_ref, rb_ref, meta_t_ref, meta_ref, counts_ref, carry_ref, *, ts):
    i = pl.program_id(0)

    @pl.when(i == 0)
    def _():
        carry_ref[...] = jnp.zeros_like(carry_ref)

    logits = jnp.dot(x_ref[...], rw_ref[...], precision=lax.Precision.HIGHEST,
                     preferred_element_type=F32) + rb_ref[...]
    lane = lax.broadcasted_iota(jnp.int32, (ts, LANES), 1)
    work = logits
    vals, idxs, sels = [], [], []
    for _ in range(TOP_K):
        m = jnp.max(work, axis=-1, keepdims=True)
        idx = jnp.min(jnp.where(work == m, lane, LANES), axis=-1, keepdims=True)
        sel = lane == idx
        vals.append(m)
        idxs.append(idx)
        sels.append(sel)
        work = jnp.where(sel, -jnp.inf, work)
    exps = [jnp.exp(v - vals[0]) for v in vals]
    denom = exps[0] + exps[1] + exps[2] + exps[3]
    selmask = jnp.where(sels[0] | sels[1] | sels[2] | sels[3], 1.0, 0.0)
    tri = (lax.broadcasted_iota(jnp.int32, (ts, ts), 0)
           > lax.broadcasted_iota(jnp.int32, (ts, ts), 1)).astype(BF16)
    rank = _dot(tri, selmask.astype(BF16)) + carry_ref[...]
    carry_ref[...] += jnp.sum(selmask, axis=0, keepdims=True)
    meta = jnp.zeros((ts, LANES), F32)
    for k in range(TOP_K):
        rank_k = jnp.sum(jnp.where(sels[k], rank, 0.0), axis=-1, keepdims=True)
        meta = jnp.where(lane == k, idxs[k].astype(F32), meta)
        meta = jnp.where(lane == TOP_K + k, rank_k, meta)
        meta = jnp.where(lane == 2 * TOP_K + k, exps[k] / denom, meta)
    meta_ref[...] = meta
    meta_t_ref[...] = meta.T[0:2 * SUBLANES, :]
    counts_ref[...] = carry_ref[...]


def _slot_rows(ref, s):
    return ref.at[pl.ds(pl.multiple_of(s * SUBLANES, SUBLANES), SUBLANES), :]


def _to_slot_tiles(ref, v, n):
    for c in range(D_MODEL // LANES):
        ref[pl.ds(c, n, stride=SUBLANES), :] = v[:, c * LANES:(c + 1) * LANES]


def _from_slot_tiles(ref, n):
    return [ref[pl.ds(c, n, stride=SUBLANES), :] for c in range(D_MODEL // LANES)]


def _dispatch_kernel(fill_start_ref, fill_cnt_ref, dest_ref, x_ref, xs_ref, xt_ref, zt_ref,
                     sem, fill_sem, *, ts):
    _to_slot_tiles(xt_ref, x_ref[...], ts)

    def issue(t, carry):
        src = _slot_rows(xt_ref, t)
        for k in range(TOP_K):
            pltpu.make_async_copy(src, _slot_rows(xs_ref, dest_ref[k, t]), sem).start()
        return carry

    lax.fori_loop(0, ts, issue, 0, unroll=4)
    for _ in range(TOP_K):
        pltpu.make_async_copy(xt_ref, xs_ref.at[pl.ds(0, ts * SUBLANES), :], sem).wait()

    @pl.when(pl.program_id(0) == pl.num_programs(0) - 1)
    def _():
        zt_ref[...] = jnp.zeros_like(zt_ref)

        def fill_expert(e, carry):
            def fill(r, c):
                pltpu.make_async_copy(zt_ref, _slot_rows(xs_ref, fill_start_ref[e] + r), fill_sem).start()
                return c

            def drain(r, c):
                pltpu.make_async_copy(zt_ref, _slot_rows(xs_ref, 0), fill_sem).wait()
                return c

            lax.fori_loop(0, fill_cnt_ref[e], fill, 0)
            lax.fori_loop(0, fill_cnt_ref[e], drain, 0)
            return carry

        lax.fori_loop(0, N_EXPERTS + 1, fill_expert, 0)


def _expert_kernel(be_ref, nused_ref, xs_ref, wgu_ref, bgu_ref, wd_ref, bd_ref, ys_ref,
                   wgu_bf_ref, wd_bf_ref):
    d = D_MODEL
    j = pl.program_id(0)

    @pl.when(j >= nused_ref[0])
    def _():
        ys_ref[...] = jnp.zeros_like(ys_ref)

    @pl.when(j < nused_ref[0])
    def _():
        @pl.when((j == 0) | (be_ref[j] != be_ref[jnp.maximum(j - 1, 0)]))
        def _():
            rows = d // SUBLANES

            def cast(r, carry):
                sl = pl.ds(pl.multiple_of(r * rows, rows), rows)
                wgu_bf_ref[sl, :] = wgu_ref[sl, :].astype(BF16)
                wd_bf_ref[sl, :] = wd_ref[sl, :].astype(BF16)
                return carry

            lax.fori_loop(0, SUBLANES, cast, 0)

        xb = jnp.concatenate(_from_slot_tiles(xs_ref, MOE_BLOCK), axis=1).astype(BF16)
        gate = _dot(xb, wgu_bf_ref[:, 0:d]) + bgu_ref[:, 0:d]
        up = _dot(xb, wgu_bf_ref[:, d:2 * d]) + bgu_ref[:, d:2 * d]
        gate = jnp.minimum(gate, SWIGLU_LIMIT)
        up = jnp.clip(up, -SWIGLU_LIMIT, SWIGLU_LIMIT)
        act = gate * jax.nn.sigmoid(SWIGLU_ALPHA * gate) * (up + 1.0)
        _to_slot_tiles(ys_ref, _dot(act.astype(BF16), wd_bf_ref[...]) + bd_ref[...], MOE_BLOCK)


def _combine_kernel(dest_ref, x_ref, meta_ref, g_ref, b_ref, ys_ref, o_ref, ybuf, sem, *, ts):
    def issue(t, carry):
        for k in range(TOP_K):
            pltpu.make_async_copy(_slot_rows(ys_ref, dest_ref[k, t]), _slot_rows(ybuf.at[k], t), sem).start()
        return carry

    lax.fori_loop(0, ts, issue, 0, unroll=4)
    for k in range(TOP_K):
        pltpu.make_async_copy(ys_ref.at[pl.ds(0, ts * SUBLANES), :], ybuf.at[k], sem).wait()
    x = x_ref[...]
    gates = [jnp.broadcast_to(meta_ref[:, 2 * TOP_K + k:2 * TOP_K + k + 1], (ts, LANES))
             for k in range(TOP_K)]
    cols = [DEEPNORM_ALPHA * x[:, c * LANES:(c + 1) * LANES] for c in range(D_MODEL // LANES)]
    for k in range(TOP_K):
        cols = [a + gates[k] * y for a, y in zip(cols, _from_slot_tiles(ybuf.at[k], ts))]
    o_ref[...] = _ln(jnp.concatenate(cols, axis=1), g_ref[...], b_ref[...])


def _params(sem):
    return pltpu.CompilerParams(dimension_semantics=sem, vmem_limit_bytes=VMEM_LIMIT)


def _const(shape):
    return pl.BlockSpec(shape, lambda *_: (0,) * len(shape))


def _row(v):
    return v.reshape(1, -1).astype(F32)


def _tile_specs(ts, nt):
    hb = ts // HALO
    last = nt // HALO - 1
    prev = pl.BlockSpec((HALO, D_MODEL), lambda i: (jnp.maximum(i * hb - 1, 0), 0))
    cur = pl.BlockSpec((ts, D_MODEL), lambda i: (i, 0))
    nxt = pl.BlockSpec((HALO, D_MODEL), lambda i: (jnp.minimum((i + 1) * hb, last), 0))
    return prev, cur, nxt


def _pool_layer(x, w_in, w_grp, scale, w_out, g, b, *, ts, n_prompt, s1, s2):
    nt, d = x.shape
    prev, cur, nxt = _tile_specs(ts, nt)
    return pl.pallas_call(
        functools.partial(_pool_kernel, ts=ts, n_prompt=n_prompt, s1=s1, s2=s2),
        out_shape=jax.ShapeDtypeStruct((nt, d), F32),
        grid=(nt // ts,),
        in_specs=[prev, cur, nxt, _const((d, d)), _const((N_GROUPS, GROUP_DIM, GROUP_DIM)),
                  _const((1, d)), _const((d, d)), _const((1, d)), _const((1, d))],
        out_specs=cur,
        compiler_params=_params(("parallel",)),
        name="pool_layer",
    )(x, x, x, w_in.astype(BF16), w_grp.astype(BF16), _row(scale), w_out.astype(BF16), _row(g), _row(b))


def _conv_layer(x, w_in, w_conv, w_out, g, b, *, ts, n_prompt, s1, s2):
    nt, d = x.shape
    prev, cur, nxt = _tile_specs(ts, nt)
    return pl.pallas_call(
        functools.partial(_conv_kernel, ts=ts, n_prompt=n_prompt, s1=s1, s2=s2),
        out_shape=jax.ShapeDtypeStruct((nt, d), F32),
        grid=(nt // ts,),
        in_specs=[prev, cur, nxt, _const((d, 3 * d)), _const((3, d)), _const((d, d)),
                  _const((1, d)), _const((1, d))],
        out_specs=cur,
        compiler_params=_params(("parallel",)),
        name="conv_layer",
    )(x, x, x, w_in.astype(BF16), w_conv.astype(F32), w_out.astype(BF16), _row(g), _row(b))


def _gmlp_layer(x, w_in, ln_g, ln_b, w_s, b_s, w_out, g, b, *, ts):
    nt, d = x.shape
    cur = pl.BlockSpec((ts, d), lambda i: (i, 0))
    return pl.pallas_call(
        functools.partial(_gmlp_kernel, ts=ts),
        out_shape=jax.ShapeDtypeStruct((nt, d), F32),
        grid=(nt // ts,),
        in_specs=[cur, _const((d, 2 * d)), _const((1, d)), _const((1, d)),
                  _const((N_GROUPS, CHUNK, CHUNK)), _const((CHUNK, N_GROUPS)), _const((d, d)),
                  _const((1, d)), _const((1, d))],
        out_specs=cur,
        compiler_params=_params(("parallel",)),
        name="gmlp_layer",
    )(x, w_in.astype(BF16), _row(ln_g), _row(ln_b), w_s.astype(BF16), b_s.T.astype(F32),
      w_out.astype(BF16), _row(g), _row(b))


def _dft_tables(n):
    j = lax.broadcasted_iota(jnp.int32, (n, n), 0)
    k = lax.broadcasted_iota(jnp.int32, (n, n), 1)
    ang = (2.0 * math.pi / n) * lax.rem(j * k, n).astype(F32)
    return jnp.cos(ang), -jnp.sin(ang)


def _fnet_layer(x, w_in, ln_g, ln_b, w_out, g, b, *, ts, segments):
    nt, d = x.shape
    cur = pl.BlockSpec((ts, d), lambda i: (i, 0))
    cc, sn_c = _dft_tables(GROUP_DIM)
    hc, hs = pl.pallas_call(
        _fnet_in_kernel,
        out_shape=(jax.ShapeDtypeStruct((nt, d), BF16), jax.ShapeDtypeStruct((nt, d), BF16)),
        grid=(nt // ts,),
        in_specs=[cur, _const((d, d)), _const((1, d)), _const((1, d)),
                  _const((GROUP_DIM, GROUP_DIM)), _const((GROUP_DIM, GROUP_DIM))],
        out_specs=(cur, cur),
        compiler_params=_params(("parallel",)),
        name="fnet_in",
    )(x, w_in.astype(BF16), _row(ln_g), _row(ln_b), cc.astype(BF16), (-sn_c).astype(BF16))
    outs = []
    for base, nb, s in segments:
        tm, tk = min(DFT_TM, s), min(DFT_TK, s)
        cs, sn = _dft_tables(s)
        mb, kb = base // tm, base // tk
        row_m = pl.BlockSpec((tm, d), lambda bi, m, k, mb=mb, s=s, tm=tm: (mb + bi * (s // tm) + m, 0))
        row_k = pl.BlockSpec((tk, d), lambda bi, m, k, kb=kb, s=s, tk=tk: (kb + bi * (s // tk) + k, 0))
        tab = pl.BlockSpec((tm, tk), lambda bi, m, k: (m, k))
        outs.append(pl.pallas_call(
            functools.partial(_fnet_seq_kernel, scale=1.0 / math.sqrt(s * GROUP_DIM)),
            out_shape=jax.ShapeDtypeStruct((nb * s, d), F32),
            grid=(nb, s // tm, s // tk),
            in_specs=[tab, tab, row_k, row_k, row_m, _const((d, d)), _const((1, d)), _const((1, d))],
            out_specs=pl.BlockSpec((tm, d), lambda bi, m, k, s=s, tm=tm: (bi * (s // tm) + m, 0)),
            scratch_shapes=[pltpu.VMEM((tm, d), F32)],
            compiler_params=_params(("parallel", "parallel", "arbitrary")),
            name="fnet_seq",
        )(cs.astype(BF16), sn.astype(BF16), hc, hs, x, w_out.astype(BF16), _row(g), _row(b)))
    return jnp.concatenate(outs, axis=0)


def _moe_layer(x, layer, router_w, router_b, w_gu, b_gu, w_down, b_down, g, b, *, ts):
    nt, d = x.shape
    n_tiles = nt // ts
    cur = pl.BlockSpec((ts, d), lambda i: (i, 0))
    rw = jnp.zeros((d, LANES), F32).at[:, :N_EXPERTS].set(router_w.astype(F32))
    rb = jnp.full((1, LANES), NEG_BIG, F32).at[0, :N_EXPERTS].set(router_b.astype(F32))
    meta_t, meta, counts = pl.pallas_call(
        functools.partial(_router_kernel, ts=ts),
        out_shape=(jax.ShapeDtypeStruct((n_tiles, 2 * SUBLANES, ts), F32),
                   jax.ShapeDtypeStruct((nt, LANES), F32),
                   jax.ShapeDtypeStruct((1, LANES), F32)),
        grid=(n_tiles,),
        in_specs=[cur, _const((d, LANES)), _const((1, LANES))],
        out_specs=(pl.BlockSpec((None, 2 * SUBLANES, ts), lambda i: (i, 0, 0)),
                   pl.BlockSpec((ts, LANES), lambda i: (i, 0)),
                   _const((1, LANES))),
        scratch_shapes=[pltpu.VMEM((1, LANES), F32)],
        compiler_params=_params(("arbitrary",)),
        name="moe_router",
    )(x, rw, rb)

    n_blocks = (nt * TOP_K) // MOE_BLOCK + N_EXPERTS
    n_slots = n_blocks * MOE_BLOCK
    cnt = counts[0, :N_EXPERTS].astype(jnp.int32)
    padded = (cnt + MOE_BLOCK - 1) // MOE_BLOCK * MOE_BLOCK
    pad_ends = jnp.cumsum(padded)
    pad_starts = pad_ends - padded
    block_start = jnp.arange(n_blocks, dtype=jnp.int32) * MOE_BLOCK
    block_e = jnp.minimum(
        jnp.sum((pad_ends[None, :] <= block_start[:, None]).astype(jnp.int32), axis=1), N_EXPERTS - 1)
    n_used = (pad_ends[-1:] // MOE_BLOCK).astype(jnp.int32)
    fill_start = jnp.concatenate([pad_starts + cnt, pad_ends[-1:]]).astype(jnp.int32)
    fill_cnt = jnp.concatenate([padded - cnt, n_slots - pad_ends[-1:]]).astype(jnp.int32)
    e_idx = meta_t[:, 0:TOP_K, :].astype(jnp.int32)
    rank = meta_t[:, TOP_K:2 * TOP_K, :].astype(jnp.int32)
    dest = rank
    for e in range(N_EXPERTS):
        dest = dest + jnp.where(e_idx == e, pad_starts[e], 0)

    any_spec = pl.BlockSpec(memory_space=pl.ANY)
    slot_shape = jax.ShapeDtypeStruct((n_slots * SUBLANES, LANES), F32)
    xs = pl.pallas_call(
        functools.partial(_dispatch_kernel, ts=ts),
        out_shape=slot_shape,
        grid_spec=pltpu.PrefetchScalarGridSpec(
            num_scalar_prefetch=2,
            grid=(n_tiles,),
            in_specs=[pl.BlockSpec((None, TOP_K, ts), lambda i, fs, fc: (i, 0, 0), memory_space=pltpu.SMEM),
                      pl.BlockSpec((ts, d), lambda i, fs, fc: (i, 0))],
            out_specs=any_spec,
            scratch_shapes=[pltpu.VMEM((ts * SUBLANES, LANES), F32), pltpu.VMEM((SUBLANES, LANES), F32),
                            pltpu.SemaphoreType.DMA, pltpu.SemaphoreType.DMA]),
        compiler_params=_params(("arbitrary",)),
        name="moe_dispatch",
    )(fill_start, fill_cnt, dest, x)

    blk_in = pl.BlockSpec((MOE_BLOCK * SUBLANES, LANES), lambda j, be, nu: (jnp.minimum(j, nu[0] - 1), 0))
    blk_out = pl.BlockSpec((MOE_BLOCK * SUBLANES, LANES), lambda j, be, nu: (j, 0))
    ys = pl.pallas_call(
        _expert_kernel,
        out_shape=slot_shape,
        grid_spec=pltpu.PrefetchScalarGridSpec(
            num_scalar_prefetch=2,
            grid=(n_blocks,),
            in_specs=[blk_in,
                      pl.BlockSpec((None, None, d, 2 * d), lambda j, be, nu: (layer, be[j], 0, 0)),
                      pl.BlockSpec((None, None, 1, 2 * d), lambda j, be, nu: (layer, be[j], 0, 0)),
                      pl.BlockSpec((None, None, d, d), lambda j, be, nu: (layer, be[j], 0, 0)),
                      pl.BlockSpec((None, None, 1, d), lambda j, be, nu: (layer, be[j], 0, 0))],
            out_specs=blk_out,
            scratch_shapes=[pltpu.VMEM((d, 2 * d), BF16), pltpu.VMEM((d, d), BF16)]),
        compiler_params=_params(("arbitrary",)),
        name="moe_experts",
    )(block_e, n_used, xs, w_gu, b_gu, w_down, b_down)

    return pl.pallas_call(
        functools.partial(_combine_kernel, ts=ts),
        out_shape=jax.ShapeDtypeStruct((nt, d), F32),
        grid=(n_tiles,),
        in_specs=[pl.BlockSpec((None, TOP_K, ts), lambda i: (i, 0, 0), memory_space=pltpu.SMEM),
                  cur, pl.BlockSpec((ts, LANES), lambda i: (i, 0)),
                  _const((1, d)), _const((1, d)), any_spec],
        out_specs=cur,
        scratch_shapes=[pltpu.VMEM((TOP_K, ts * SUBLANES, LANES), F32), pltpu.SemaphoreType.DMA],
        compiler_params=_params(("arbitrary",)),
        name="moe_combine",
    )(dest, x, meta, _row(g), _row(b), ys)


def kernel(x_prompt, x_sample, pool_w_in, pool_w_grp, pool_scale, pool_w_out, conv_w_in, conv_w, conv_w_out, gmlp_w_in, gmlp_ln_g, gmlp_ln_b, gmlp_w_s, gmlp_b_s, gmlp_w_out, fnet_w_in, fnet_ln_g, fnet_ln_b, fnet_w_out, ln1_g, ln1_b, ln2_g, ln2_b, router_w, router_b, moe_w_gu, moe_b_gu, moe_w_down, moe_b_down):
    b1, s1, d = x_prompt.shape
    b2, s2, _ = x_sample.shape
    assert d == D_MODEL and ln1_g.shape[0] == DEPTH
    n_prompt = b1 * s1
    ts = math.gcd(TILE_ROWS, math.gcd(s1, s2))
    assert ts % CHUNK == 0
    x = jnp.concatenate([x_prompt.reshape(n_prompt, d), x_sample.reshape(b2 * s2, d)], axis=0)
    seq = dict(ts=ts, n_prompt=n_prompt, s1=s1, s2=s2)
    segments = ((0, b1, s1), (n_prompt, b2, s2))
    w_gu = moe_w_gu.astype(F32)
    w_down = moe_w_down.astype(F32)
    b_gu = moe_b_gu.astype(F32)[:, :, None, :]
    b_down = moe_b_down.astype(F32)[:, :, None, :]
    for i in range(DEPTH):
        m, j = i % 4, i // 4
        if m == 0:
            x = _pool_layer(x, pool_w_in[j], pool_w_grp[j], pool_scale[j], pool_w_out[j],
                            ln1_g[i], ln1_b[i], **seq)
        elif m == 1:
            x = _conv_layer(x, conv_w_in[j], conv_w[j], conv_w_out[j], ln1_g[i], ln1_b[i], **seq)
        elif m == 2:
            x = _gmlp_layer(x, gmlp_w_in[j], gmlp_ln_g[j], gmlp_ln_b[j], gmlp_w_s[j], gmlp_b_s[j],
                            gmlp_w_out[j], ln1_g[i], ln1_b[i], ts=ts)
        else:
            x = _fnet_layer(x, fnet_w_in[j], fnet_ln_g[j], fnet_ln_b[j], fnet_w_out[j],
                            ln1_g[i], ln1_b[i], ts=ts, segments=segments)
        x = _moe_layer(x, i, router_w[i], router_b[i], w_gu, b_gu, w_down, b_down,
                       ln2_g[i], ln2_b[i], ts=ts)
    return (x[:n_prompt].reshape(b1, s1, d), x[n_prompt:].reshape(b2, s2, d))
```

```python
import functools
import math

import jax
import jax.numpy as jnp
from jax import lax
from jax.experimental import pallas as pl
from jax.experimental.pallas import tpu as pltpu

D_MODEL = 1024
DEPTH = 4
N_GROUPS = 4
GROUP_DIM = D_MODEL // N_GROUPS
POOL_WINDOWS = (2, 4, 8, 16)
CHUNK = 128
N_EXPERTS = 32
TOP_K = 4
SWIGLU_ALPHA = 1.702
SWIGLU_LIMIT = 7.0
LN_EPS = 1e-5
DEEPNORM_ALPHA = (2 * DEPTH) ** 0.25

LANES = 128
SUBLANES = 8
HALO = SUBLANES
TILE_ROWS = 512
MOE_BLOCK = 512
DFT_TM = 512
DFT_TK = 512
VMEM_LIMIT = 56 * 1024 * 1024
NEG_BIG = -1e30

F32 = jnp.float32
BF16 = jnp.bfloat16


def _ln(v, g, b):
    mu = jnp.mean(v, axis=-1, keepdims=True)
    d = v - mu
    var = jnp.mean(d * d, axis=-1, keepdims=True)
    return d * lax.rsqrt(var + LN_EPS) * g + b


def _dot(a, b):
    return jnp.dot(a, b, preferred_element_type=F32)


def _seq_pos(i, ts, n_prompt, s1, s2):
    row0 = i * ts
    in_prompt = row0 < n_prompt
    s = jnp.where(in_prompt, s1, s2)
    pos0 = jnp.where(in_prompt, lax.rem(row0, s1), lax.rem(row0 - n_prompt, s2))
    return s, pos0


def _ext_rows(xp_ref, xc, xn_ref):
    return jnp.concatenate([xp_ref[...], xc, xn_ref[...]], axis=0).astype(BF16)


def _ext_positions(pos0, n_ext):
    return lax.broadcasted_iota(jnp.int32, (n_ext, 1), 0) + (pos0 - HALO)


def _pool_kernel(xp_ref, xc_ref, xn_ref, win_ref, wgrp_ref, scale_ref, wout_ref,
                 g_ref, b_ref, o_ref, *, ts, n_prompt, s1, s2):
    s, pos0 = _seq_pos(pl.program_id(0), ts, n_prompt, s1, s2)
    n_ext = ts + 2 * HALO
    xc = xc_ref[...]
    h = _dot(_ext_rows(xp_ref, xc, xn_ref), win_ref[...])
    r = _ext_positions(pos0, n_ext)
    h = jnp.where((r >= 0) & (r < s), h, 0.0)
    pos = r[HALO:HALO + ts]
    outs = []
    for g, w in enumerate(POOL_WINDOWS):
        hg = h[:, g * GROUP_DIM:(g + 1) * GROUP_DIM]
        acc = hg + pltpu.roll(hg, 1, 0)
        step = 1
        while 2 * step < w:
            acc = pltpu.roll(acc, step, 0) + pltpu.roll(acc, n_ext - step, 0)
            step *= 2
        lo = jnp.maximum(pos - w // 2, 0)
        hi = jnp.minimum(pos + w // 2 - 1, s - 1)
        cnt = (hi - lo + 1).astype(F32)
        pooled = acc[HALO:HALO + ts] / cnt - hg[HALO:HALO + ts]
        outs.append(_dot(pooled.astype(BF16), wgrp_ref[g]))
    mixed = jnp.concatenate(outs, axis=1) * scale_ref[...]
    t = _dot(mixed.astype(BF16), wout_ref[...])
    o_ref[...] = _ln(DEEPNORM_ALPHA * xc + t, g_ref[...], b_ref[...])


def _conv_kernel(xp_ref, xc_ref, xn_ref, win_ref, wconv_ref, wout_ref,
                 g_ref, b_ref, o_ref, *, ts, n_prompt, s1, s2):
    d = D_MODEL
    s, pos0 = _seq_pos(pl.program_id(0), ts, n_prompt, s1, s2)
    n_ext = ts + 2 * HALO
    xc = xc_ref[...]
    xe = _ext_rows(xp_ref, xc, xn_ref)
    r = _ext_positions(pos0, n_ext)
    cg = _dot(xe, win_ref[:, d:2 * d])
    hh = _dot(xe, win_ref[:, 2 * d:3 * d])
    z = jnp.where((r >= 0) & (r < s), cg * hh, 0.0)
    c = slice(HALO, HALO + ts)
    conv = (pltpu.roll(z, 1, 0)[c] * wconv_ref[0:1, :] + z[c] * wconv_ref[1:2, :]
            + pltpu.roll(z, n_ext - 1, 0)[c] * wconv_ref[2:3, :])
    bg = _dot(xc.astype(BF16), win_ref[:, 0:d])
    t = _dot((bg * conv).astype(BF16), wout_ref[...])
    o_ref[...] = _ln(DEEPNORM_ALPHA * xc + t, g_ref[...], b_ref[...])


def _gmlp_kernel(xc_ref, win_ref, lng_ref, lnb_ref, ws_ref, bs_ref, wout_ref,
                 g_ref, b_ref, o_ref, *, ts):
    d = D_MODEL
    xc = xc_ref[...]
    xb = xc.astype(BF16)

    def gelu(a):
        return 0.5 * a * (1.0 + lax.erf(a * (1.0 / math.sqrt(2.0))))

    u = gelu(_dot(xb, win_ref[:, 0:d]))
    v = gelu(_dot(xb, win_ref[:, d:2 * d]))
    v = _ln(v, lng_ref[...], lnb_ref[...]).astype(BF16)
    cols = []
    for g in range(N_GROUPS):
        w_s = ws_ref[g]
        b_s = bs_ref[:, g:g + 1]
        rows = []
        for c in range(ts // CHUNK):
            vc = v[c * CHUNK:(c + 1) * CHUNK, g * GROUP_DIM:(g + 1) * GROUP_DIM]
            rows.append(_dot(w_s, vc) + b_s)
        cols.append(jnp.concatenate(rows, axis=0))
    vs = jnp.concatenate(cols, axis=1)
    t = _dot((u * vs).astype(BF16), wout_ref[...])
    o_ref[...] = _ln(DEEPNORM_ALPHA * xc + t, g_ref[...], b_ref[...])


def _fnet_in_kernel(xc_ref, win_ref, lng_ref, lnb_ref, cc_ref, sc_ref, hc_ref, hs_ref):
    h = _dot(xc_ref[...].astype(BF16), win_ref[...])
    for g in range(N_GROUPS):
        sl = slice(g * GROUP_DIM, (g + 1) * GROUP_DIM)
        hn = _ln(h[:, sl], lng_ref[:, sl], lnb_ref[:, sl]).astype(BF16)
        hc_ref[:, sl] = _dot(hn, cc_ref[...]).astype(BF16)
        hs_ref[:, sl] = _dot(hn, sc_ref[...]).astype(BF16)


def _fnet_seq_kernel(cs_ref, sn_ref, hc_ref, hs_ref, x_ref, wout_ref, g_ref, b_ref,
                     o_ref, acc_ref, *, scale):
    k = pl.program_id(2)

    @pl.when(k == 0)
    def _():
        acc_ref[...] = jnp.zeros_like(acc_ref)

    acc_ref[...] += _dot(cs_ref[...], hc_ref[...]) + _dot(sn_ref[...], hs_ref[...])

    @pl.when(k == pl.num_programs(2) - 1)
    def _():
        f = acc_ref[...] * scale
        t = _dot(f.astype(BF16), wout_ref[...])
        o_ref[...] = _ln(DEEPNORM_ALPHA * x_ref[...] + t, g_ref[...], b_ref[...])


def _router_kernel(x_ref, rw_ref, rb_ref, meta_t_ref, meta_ref, counts_ref, carry_ref, *, ts):
    i = pl.program_id(0)

    @pl.when(i == 0)
    def _():
        carry_ref[...] = jnp.zeros_like(carry_ref)

    x = x_ref[...]
    xh = x.astype(BF16)
    xl = (x - xh.astype(F32)).astype(BF16)
    hi = _dot(xh, rw_ref[...])
    logits = hi[:, 0:LANES] + (hi[:, LANES:2 * LANES] + _dot(xl, rw_ref[:, 0:LANES])) + rb_ref[...]
    lane = lax.broadcasted_iota(jnp.int32, (ts, LANES), 1)
    work = logits
    vals, idxs, sels = [], [], []
    for _ in range(TOP_K):
        m = jnp.max(work, axis=-1, keepdims=True)
        idx = jnp.min(jnp.where(work == m, lane, LANES), axis=-1, keepdims=True)
        sel = lane == idx
        vals.append(m)
        idxs.append(idx)
        sels.append(sel)
        work = jnp.where(sel, -jnp.inf, work)
    exps = [jnp.exp(v - vals[0]) for v in vals]
    denom = exps[0] + exps[1] + exps[2] + exps[3]
    selmask = jnp.where(sels[0] | sels[1] | sels[2] | sels[3], 1.0, 0.0)
    tri = (lax.broadcasted_iota(jnp.int32, (ts, ts), 0)
           > lax.broadcasted_iota(jnp.int32, (ts, ts), 1)).astype(BF16)
    rank = _dot(tri, selmask.astype(BF16)) + carry_ref[...]
    carry_ref[...] += jnp.sum(selmask, axis=0, keepdims=True)
    meta = jnp.zeros((ts, LANES), F32)
    for k in range(TOP_K):
        rank_k = jnp.sum(jnp.where(sels[k], rank, 0.0), axis=-1, keepdims=True)
        meta = jnp.where(lane == k, idxs[k].astype(F32), meta)
        meta = jnp.where(lane == TOP_K + k, rank_k, meta)
        meta = jnp.where(lane == 2 * TOP_K + k, exps[k] / denom, meta)
    meta_ref[...] = meta
    meta_t_ref[...] = meta.T[0:2 * SUBLANES, :]
    counts_ref[...] = carry_ref[...]


def _slot_rows(ref, s):
    return ref.at[pl.ds(pl.multiple_of(s * SUBLANES, SUBLANES), SUBLANES), :]


def _to_slot_tiles(ref, v, n):
    for c in range(D_MODEL // LANES):
        ref[pl.ds(c, n, stride=SUBLANES), :] = v[:, c * LANES:(c + 1) * LANES]


def _from_slot_tiles(ref, n):
    return [ref[pl.ds(c, n, stride=SUBLANES), :] for c in range(D_MODEL // LANES)]


def _dispatch_kernel(fill_start_ref, fill_cnt_ref, dest_ref, x_ref, xs_ref, xt_ref, zt_ref,
                     sem, fill_sem, *, ts):
    _to_slot_tiles(xt_ref, x_ref[...], ts)

    def issue(t, carry):
        src = _slot_rows(xt_ref, t)
        for k in range(TOP_K):
            pltpu.make_async_copy(src, _slot_rows(xs_ref, dest_ref[k, t]), sem).start(priority=k % 2)
        return carry

    lax.fori_loop(0, ts, issue, 0, unroll=4)
    for _ in range(TOP_K):
        pltpu.make_async_copy(xt_ref, xs_ref.at[pl.ds(0, ts * SUBLANES), :], sem).wait()

    @pl.when(pl.program_id(0) == pl.num_programs(0) - 1)
    def _():
        zt_ref[...] = jnp.zeros_like(zt_ref)

        def fill_expert(e, carry):
            def fill(r, c):
                pltpu.make_async_copy(zt_ref, _slot_rows(xs_ref, fill_start_ref[e] + r), fill_sem).start()
                return c

            def drain(r, c):
                pltpu.make_async_copy(zt_ref, _slot_rows(xs_ref, 0), fill_sem).wait()
                return c

            lax.fori_loop(0, fill_cnt_ref[e], fill, 0)
            lax.fori_loop(0, fill_cnt_ref[e], drain, 0)
            return carry

        lax.fori_loop(0, N_EXPERTS + 1, fill_expert, 0)


def _expert_kernel(be_ref, nused_ref, xs_ref, wgu_ref, bgu_ref, wd_ref, bd_ref, ys_ref,
                   wgu_bf_ref, wd_bf_ref):
    d = D_MODEL
    j = pl.program_id(0)

    @pl.when(j >= nused_ref[0])
    def _():
        ys_ref[...] = jnp.zeros_like(ys_ref)

    @pl.when(j < nused_ref[0])
    def _():
        @pl.when((j == 0) | (be_ref[j] != be_ref[jnp.maximum(j - 1, 0)]))
        def _():
            rows = d // SUBLANES

            def cast(r, carry):
                sl = pl.ds(pl.multiple_of(r * rows, rows), rows)
                wgu_bf_ref[sl, :] = wgu_ref[sl, :].astype(BF16)
                wd_bf_ref[sl, :] = wd_ref[sl, :].astype(BF16)
                return carry

            lax.fori_loop(0, SUBLANES, cast, 0)

        xb = jnp.concatenate(_from_slot_tiles(xs_ref, MOE_BLOCK), axis=1).astype(BF16)
        gate = _dot(xb, wgu_bf_ref[:, 0:d]) + bgu_ref[:, 0:d]
        up = _dot(xb, wgu_bf_ref[:, d:2 * d]) + bgu_ref[:, d:2 * d]
        gate = jnp.minimum(gate, SWIGLU_LIMIT)
        up = jnp.clip(up, -SWIGLU_LIMIT, SWIGLU_LIMIT)
        act = gate * jax.nn.sigmoid(SWIGLU_ALPHA * gate) * (up + 1.0)
        _to_slot_tiles(ys_ref, _dot(act.astype(BF16), wd_bf_ref[...]) + bd_ref[...], MOE_BLOCK)


def _combine_kernel(dest_ref, x_ref, meta_ref, g_ref, b_ref, ys_ref, o_ref, ybuf, sem, *, ts):
    def issue(t, carry):
        for k in range(TOP_K):
            pltpu.make_async_copy(_slot_rows(ys_ref, dest_ref[k, t]), _slot_rows(ybuf.at[k], t),
                                  sem).start(priority=k % 2)
        return carry

    lax.fori_loop(0, ts, issue, 0, unroll=4)
    for k in range(TOP_K):
        pltpu.make_async_copy(ys_ref.at[pl.ds(0, ts * SUBLANES), :], ybuf.at[k], sem).wait()
    x = x_ref[...]
    gates = [jnp.broadcast_to(meta_ref[:, 2 * TOP_K + k:2 * TOP_K + k + 1], (ts, LANES))
             for k in range(TOP_K)]
    cols = [DEEPNORM_ALPHA * x[:, c * LANES:(c + 1) * LANES] for c in range(D_MODEL // LANES)]
    for k in range(TOP_K):
        cols = [a + gates[k] * y for a, y in zip(cols, _from_slot_tiles(ybuf.at[k], ts))]
    o_ref[...] = _ln(jnp.concatenate(cols, axis=1), g_ref[...], b_ref[...])


def _params(sem):
    return pltpu.CompilerParams(dimension_semantics=sem, vmem_limit_bytes=VMEM_LIMIT)


def _const(shape):
    return pl.BlockSpec(shape, lambda *_: (0,) * len(shape))


def _row(v):
    return v.reshape(1, -1).astype(F32)


def _tile_specs(ts, nt):
    hb = ts // HALO
    last = nt // HALO - 1
    prev = pl.BlockSpec((HALO, D_MODEL), lambda i: (jnp.maximum(i * hb - 1, 0), 0))
    cur = pl.BlockSpec((ts, D_MODEL), lambda i: (i, 0))
    nxt = pl.BlockSpec((HALO, D_MODEL), lambda i: (jnp.minimum((i + 1) * hb, last), 0))
    return prev, cur, nxt


def _pool_layer(x, w_in, w_grp, scale, w_out, g, b, *, ts, n_prompt, s1, s2):
    nt, d = x.shape
    prev, cur, nxt = _tile_specs(ts, nt)
    return pl.pallas_call(
        functools.partial(_pool_kernel, ts=ts, n_prompt=n_prompt, s1=s1, s2=s2),
        out_shape=jax.ShapeDtypeStruct((nt, d), F32),
        grid=(nt // ts,),
        in_specs=[prev, cur, nxt, _const((d, d)), _const((N_GROUPS, GROUP_DIM, GROUP_DIM)),
                  _const((1, d)), _const((d, d)), _const((1, d)), _const((1, d))],
        out_specs=cur,
        compiler_params=_params(("parallel",)),
        name="pool_layer",
    )(x, x, x, w_in.astype(BF16), w_grp.astype(BF16), _row(scale), w_out.astype(BF16), _row(g), _row(b))


def _conv_layer(x, w_in, w_conv, w_out, g, b, *, ts, n_prompt, s1, s2):
    nt, d = x.shape
    prev, cur, nxt = _tile_specs(ts, nt)
    return pl.pallas_call(
        functools.partial(_conv_kernel, ts=ts, n_prompt=n_prompt, s1=s1, s2=s2),
        out_shape=jax.ShapeDtypeStruct((nt, d), F32),
        grid=(nt // ts,),
        in_specs=[prev, cur, nxt, _const((d, 3 * d)), _const((3, d)), _const((d, d)),
                  _const((1, d)), _const((1, d))],
        out_specs=cur,
        compiler_params=_params(("parallel",)),
        name="conv_layer",
    )(x, x, x, w_in.astype(BF16), w_conv.astype(F32), w_out.astype(BF16), _row(g), _row(b))


def _gmlp_layer(x, w_in, ln_g, ln_b, w_s, b_s, w_out, g, b, *, ts):
    nt, d = x.shape
    cur = pl.BlockSpec((ts, d), lambda i: (i, 0))
    return pl.pallas_call(
        functools.partial(_gmlp_kernel, ts=ts),
        out_shape=jax.ShapeDtypeStruct((nt, d), F32),
        grid=(nt // ts,),
        in_specs=[cur, _const((d, 2 * d)), _const((1, d)), _const((1, d)),
                  _const((N_GROUPS, CHUNK, CHUNK)), _const((CHUNK, N_GROUPS)), _const((d, d)),
                  _const((1, d)), _const((1, d))],
        out_specs=cur,
        compiler_params=_params(("parallel",)),
        name="gmlp_layer",
    )(x, w_in.astype(BF16), _row(ln_g), _row(ln_b), w_s.astype(BF16), b_s.T.astype(F32),
      w_out.astype(BF16), _row(g), _row(b))


def _dft_tables(n):
    j = lax.broadcasted_iota(jnp.int32, (n, n), 0)
    k = lax.broadcasted_iota(jnp.int32, (n, n), 1)
    ang = (2.0 * math.pi / n) * lax.rem(j * k, n).astype(F32)
    return jnp.cos(ang), -jnp.sin(ang)


def _fnet_layer(x, w_in, ln_g, ln_b, w_out, g, b, *, ts, segments):
    nt, d = x.shape
    cur = pl.BlockSpec((ts, d), lambda i: (i, 0))
    cc, sn_c = _dft_tables(GROUP_DIM)
    hc, hs = pl.pallas_call(
        _fnet_in_kernel,
        out_shape=(jax.ShapeDtypeStruct((nt, d), BF16), jax.ShapeDtypeStruct((nt, d), BF16)),
        grid=(nt // ts,),
        in_specs=[cur, _const((d, d)), _const((1, d)), _const((1, d)),
                  _const((GROUP_DIM, GROUP_DIM)), _const((GROUP_DIM, GROUP_DIM))],
        out_specs=(cur, cur),
        compiler_params=_params(("parallel",)),
        name="fnet_in",
    )(x, w_in.astype(BF16), _row(ln_g), _row(ln_b), cc.astype(BF16), (-sn_c).astype(BF16))
    outs = []
    for base, nb, s in segments:
        tm, tk = min(DFT_TM, s), min(DFT_TK, s)
        cs, sn = _dft_tables(s)
        mb, kb = base // tm, base // tk
        row_m = pl.BlockSpec((tm, d), lambda bi, m, k, mb=mb, s=s, tm=tm: (mb + bi * (s // tm) + m, 0))
        row_k = pl.BlockSpec((tk, d), lambda bi, m, k, kb=kb, s=s, tk=tk: (kb + bi * (s // tk) + k, 0))
        tab = pl.BlockSpec((tm, tk), lambda bi, m, k: (m, k))
        outs.append(pl.pallas_call(
            functools.partial(_fnet_seq_kernel, scale=1.0 / math.sqrt(s * GROUP_DIM)),
            out_shape=jax.ShapeDtypeStruct((nb * s, d), F32),
            grid=(nb, s // tm, s // tk),
            in_specs=[tab, tab, row_k, row_k, row_m, _const((d, d)), _const((1, d)), _const((1, d))],
            out_specs=pl.BlockSpec((tm, d), lambda bi, m, k, s=s, tm=tm: (bi * (s // tm) + m, 0)),
            scratch_shapes=[pltpu.VMEM((tm, d), F32)],
            compiler_params=_params(("parallel", "parallel", "arbitrary")),
            name="fnet_seq",
        )(cs.astype(BF16), sn.astype(BF16), hc, hs, x, w_out.astype(BF16), _row(g), _row(b)))
    return jnp.concatenate(outs, axis=0)


def _moe_layer(x, layer, router_w, router_b, w_gu, b_gu, w_down, b_down, g, b, *, ts):
    nt, d = x.shape
    n_tiles = nt // ts
    cur = pl.BlockSpec((ts, d), lambda i: (i, 0))
    rw = jnp.zeros((d, LANES), F32).at[:, :N_EXPERTS].set(router_w.astype(F32))
    rw_hi = rw.astype(BF16)
    rw = jnp.concatenate([rw_hi, (rw - rw_hi.astype(F32)).astype(BF16)], axis=1)
    rb = jnp.full((1, LANES), NEG_BIG, F32).at[0, :N_EXPERTS].set(router_b.astype(F32))
    meta_t, meta, counts = pl.pallas_call(
        functools.partial(_router_kernel, ts=ts),
        out_shape=(jax.ShapeDtypeStruct((n_tiles, 2 * SUBLANES, ts), F32),
                   jax.ShapeDtypeStruct((nt, LANES), F32),
                   jax.ShapeDtypeStruct((1, LANES), F32)),
        grid=(n_tiles,),
        in_specs=[cur, _const((d, 2 * LANES)), _const((1, LANES))],
        out_specs=(pl.BlockSpec((None, 2 * SUBLANES, ts), lambda i: (i, 0, 0)),
                   pl.BlockSpec((ts, LANES), lambda i: (i, 0)),
                   _const((1, LANES))),
        scratch_shapes=[pltpu.VMEM((1, LANES), F32)],
        compiler_params=_params(("arbitrary",)),
        name="moe_router",
    )(x, rw, rb)

    n_blocks = (nt * TOP_K) // MOE_BLOCK + N_EXPERTS
    n_slots = n_blocks * MOE_BLOCK
    cnt = counts[0, :N_EXPERTS].astype(jnp.int32)
    padded = (cnt + MOE_BLOCK - 1) // MOE_BLOCK * MOE_BLOCK
    pad_ends = jnp.cumsum(padded)
    pad_starts = pad_ends - padded
    block_start = jnp.arange(n_blocks, dtype=jnp.int32) * MOE_BLOCK
    block_e = jnp.minimum(
        jnp.sum((pad_ends[None, :] <= block_start[:, None]).astype(jnp.int32), axis=1), N_EXPERTS - 1)
    n_used = (pad_ends[-1:] // MOE_BLOCK).astype(jnp.int32)
    fill_start = jnp.concatenate([pad_starts + cnt, pad_ends[-1:]]).astype(jnp.int32)
    fill_cnt = jnp.concatenate([padded - cnt, n_slots - pad_ends[-1:]]).astype(jnp.int32)
    e_idx = meta_t[:, 0:TOP_K, :].astype(jnp.int32)
    rank = meta_t[:, TOP_K:2 * TOP_K, :].astype(jnp.int32)
    dest = rank
    for e in range(N_EXPERTS):
        dest = dest + jnp.where(e_idx == e, pad_starts[e], 0)

    any_spec = pl.BlockSpec(memory_space=pl.ANY)
    slot_shape = jax.ShapeDtypeStruct((n_slots * SUBLANES, LANES), F32)
    xs = pl.pallas_call(
        functools.partial(_dispatch_kernel, ts=ts),
        out_shape=slot_shape,
        grid_spec=pltpu.PrefetchScalarGridSpec(
            num_scalar_prefetch=2,
            grid=(n_tiles,),
            in_specs=[pl.BlockSpec((None, TOP_K, ts), lambda i, fs, fc: (i, 0, 0), memory_space=pltpu.SMEM),
                      pl.BlockSpec((ts, d), lambda i, fs, fc: (i, 0))],
            out_specs=any_spec,
            scratch_shapes=[pltpu.VMEM((ts * SUBLANES, LANES), F32), pltpu.VMEM((SUBLANES, LANES), F32),
                            pltpu.SemaphoreType.DMA, pltpu.SemaphoreType.DMA]),
        compiler_params=_params(("arbitrary",)),
        name="moe_dispatch",
    )(fill_start, fill_cnt, dest, x)

    blk_in = pl.BlockSpec((MOE_BLOCK * SUBLANES, LANES), lambda j, be, nu: (jnp.minimum(j, nu[0] - 1), 0))
    blk_out = pl.BlockSpec((MOE_BLOCK * SUBLANES, LANES), lambda j, be, nu: (j, 0))
    ys = pl.pallas_call(
        _expert_kernel,
        out_shape=slot_shape,
        grid_spec=pltpu.PrefetchScalarGridSpec(
            num_scalar_prefetch=2,
            grid=(n_blocks,),
            in_specs=[blk_in,
                      pl.BlockSpec((None, None, d, 2 * d), lambda j, be, nu: (layer, be[j], 0, 0)),
                      pl.BlockSpec((None, None, 1, 2 * d), lambda j, be, nu: (layer, be[j], 0, 0)),
                      pl.BlockSpec((None, None, d, d), lambda j, be, nu: (layer, be[j], 0, 0)),
                      pl.BlockSpec((None, None, 1, d), lambda j, be, nu: (layer, be[j], 0, 0))],
            out_specs=blk_out,
            scratch_shapes=[pltpu.VMEM((d, 2 * d), BF16), pltpu.VMEM((d, d), BF16)]),
        compiler_params=_params(("arbitrary",)),
        name="moe_experts",
    )(block_e, n_used, xs, w_gu, b_gu, w_down, b_down)

    return pl.pallas_call(
        functools.partial(_combine_kernel, ts=ts),
        out_shape=jax.ShapeDtypeStruct((nt, d), F32),
        grid=(n_tiles,),
        in_specs=[pl.BlockSpec((None, TOP_K, ts), lambda i: (i, 0, 0), memory_space=pltpu.SMEM),
                  cur, pl.BlockSpec((ts, LANES), lambda i: (i, 0)),
                  _const((1, d)), _const((1, d)), any_spec],
        out_specs=cur,
        scratch_shapes=[pltpu.VMEM((TOP_K, ts * SUBLANES, LANES), F32), pltpu.SemaphoreType.DMA],
        compiler_params=_params(("arbitrary",)),
        name="moe_combine",
    )(dest, x, meta, _row(g), _row(b), ys)


def kernel(x_prompt, x_sample, pool_w_in, pool_w_grp, pool_scale, pool_w_out, conv_w_in, conv_w, conv_w_out, gmlp_w_in, gmlp_ln_g, gmlp_ln_b, gmlp_w_s, gmlp_b_s, gmlp_w_out, fnet_w_in, fnet_ln_g, fnet_ln_b, fnet_w_out, ln1_g, ln1_b, ln2_g, ln2_b, router_w, router_b, moe_w_gu, moe_b_gu, moe_w_down, moe_b_down):
    b1, s1, d = x_prompt.shape
    b2, s2, _ = x_sample.shape
    assert d == D_MODEL and ln1_g.shape[0] == DEPTH
    n_prompt = b1 * s1
    ts = math.gcd(TILE_ROWS, math.gcd(s1, s2))
    assert ts % CHUNK == 0
    x = jnp.concatenate([x_prompt.reshape(n_prompt, d), x_sample.reshape(b2 * s2, d)], axis=0)
    seq = dict(ts=ts, n_prompt=n_prompt, s1=s1, s2=s2)
    segments = ((0, b1, s1), (n_prompt, b2, s2))
    w_gu = moe_w_gu.astype(F32)
    w_down = moe_w_down.astype(F32)
    b_gu = moe_b_gu.astype(F32)[:, :, None, :]
    b_down = moe_b_down.astype(F32)[:, :, None, :]
    for i in range(DEPTH):
        m, j = i % 4, i // 4
        if m == 0:
            x = _pool_layer(x, pool_w_in[j], pool_w_grp[j], pool_scale[j], pool_w_out[j],
                            ln1_g[i], ln1_b[i], **seq)
        elif m == 1:
            x = _conv_layer(x, conv_w_in[j], conv_w[j], conv_w_out[j], ln1_g[i], ln1_b[i], **seq)
        elif m == 2:
            x = _gmlp_layer(x, gmlp_w_in[j], gmlp_ln_g[j], gmlp_ln_b[j], gmlp_w_s[j], gmlp_b_s[j],
                            gmlp_w_out[j], ln1_g[i], ln1_b[i], ts=ts)
        else:
            x = _fnet_layer(x, fnet_w_in[j], fnet_ln_g[j], fnet_ln_b[j], fnet_w_out[j],
                            ln1_g[i], ln1_b[i], ts=ts, segments=segments)
        x = _moe_layer(x, i, router_w[i], router_b[i], w_gu, b_gu, w_down, b_down,
                       ln2_g[i], ln2_b[i], ts=ts)
    return (x[:n_prompt].reshape(b1, s1, d), x[n_prompt:].reshape(b2, s2, d))
```
